```python
import jax, jax.numpy as jnp
from jax import lax
import numpy as np


D_MODEL = 2048
BATCH = 8
SEQ = 2048
DEPTH = 1
DEC_BATCH = 2
DEC_SEQ = 8192
PAST_LEN = 128

D_CONV = D_MODEL // 2
CONV_WIDTH = 31
N_HEADS = 8
QK_NOPE = 128
QK_ROPE = 64
V_DIM = 128
Q_LORA = 512
KV_LORA = 512
D_ATTN = N_HEADS * V_DIM
D_MIX = D_CONV + D_ATTN
D_IN = 2 * D_CONV + Q_LORA + KV_LORA + QK_ROPE
D_FF = -(-8 * D_MODEL // (3 * 256)) * 256
Q_BLOCK = 128
ROPE_BASE = 10000.0
EPS = 1e-6

kernel_name = 'hybrid_conv_mla_adaln_encoder'


def _rmsnorm(x, g):
    xf = x.astype(jnp.float32)
    y = xf * lax.rsqrt(jnp.mean(xf * xf, axis=-1, keepdims=True) + EPS)
    return (y * g.astype(jnp.float32)).astype(x.dtype)


def _layernorm(x, g, b):
    xf = x.astype(jnp.float32)
    mu = jnp.mean(xf, axis=-1, keepdims=True)
    xc = xf - mu
    y = xc * lax.rsqrt(jnp.mean(xc * xc, axis=-1, keepdims=True) + EPS)
    return (y * g.astype(jnp.float32) + b.astype(jnp.float32)).astype(x.dtype)


def _rope_tables(seq):
    inv = 1.0 / (ROPE_BASE ** (jnp.arange(0, QK_ROPE, 2, dtype=jnp.float32) / QK_ROPE))
    ang = jnp.arange(seq, dtype=jnp.float32)[:, None] * inv[None, :]
    return jnp.cos(ang), jnp.sin(ang)


def _rotate(x, cos, sin):
    x1, x2 = jnp.split(x.astype(jnp.float32), 2, axis=-1)
    return jnp.concatenate([x1 * cos - x2 * sin, x1 * sin + x2 * cos], axis=-1).astype(x.dtype)


def _conv_group(u, w_dw, b_dw, g_ln, b_ln):
    a, gate = jnp.split(u, 2, axis=-1)
    h = a * jax.nn.sigmoid(gate)
    h = lax.conv_general_dilated(
        h, w_dw[:, None, :], window_strides=(1,),
        padding=[(CONV_WIDTH // 2, CONV_WIDTH // 2)],
        dimension_numbers=('NWC', 'WIO', 'NWC'),
        feature_group_count=D_CONV) + b_dw
    return jax.nn.silu(_layernorm(h, g_ln, b_ln))


def _mla_attention(q_nope, q_rope, k_nope, k_rope, v):
    b, s, h, _ = q_nope.shape
    nb = s // Q_BLOCK
    scale = (QK_NOPE + QK_ROPE) ** -0.5

    def blocks(t):
        return jnp.moveaxis(t.reshape(b, nb, Q_BLOCK, *t.shape[2:]), 1, 0)

    def one_block(qs):
        qn, qr = qs
        sc = (jnp.einsum('bqhd,bkhd->bhqk', qn, k_nope, preferred_element_type=jnp.float32)
              + jnp.einsum('bqhr,bkr->bhqk', qr, k_rope, preferred_element_type=jnp.float32))
        p = jax.nn.softmax(sc * scale, axis=-1).astype(v.dtype)
        return jnp.einsum('bhqk,bkhd->bqhd', p, v)

    out = lax.map(one_block, (blocks(q_nope), blocks(q_rope)))
    return jnp.moveaxis(out, 0, 1).reshape(b, s, h * V_DIM)


def _layer(x, c, w_ada, b_ada, g_pre_mix, g_post_mix, w_in, w_dw, b_dw, g_conv, b_conv,
           g_q_lat, w_uq, g_kv_lat, w_ukv, w_out, g_pre_ffn, g_post_ffn, w_gate_up, w_down):
    b, s, _ = x.shape
    mod = jax.nn.silu(c) @ w_ada + b_ada
    sh_m, sc_m, gt_m, sh_f, sc_f, gt_f = jnp.split(mod[:, None, :], 6, axis=-1)

    h = _rmsnorm(x, g_pre_mix) * (1 + sc_m) + sh_m
    u = h @ w_in
    u_conv, q_lat, kv_lat, k_rope = jnp.split(
        u, [2 * D_CONV, 2 * D_CONV + Q_LORA, 2 * D_CONV + Q_LORA + KV_LORA], axis=-1)

    conv_out = _conv_group(u_conv, w_dw, b_dw, g_conv, b_conv)

    q = (_rmsnorm(q_lat, g_q_lat) @ w_uq).reshape(b, s, N_HEADS, QK_NOPE + QK_ROPE)
    q_nope, q_rope = jnp.split(q, [QK_NOPE], axis=-1)
    kv = (_rmsnorm(kv_lat, g_kv_lat) @ w_ukv).reshape(b, s, N_HEADS, QK_NOPE + V_DIM)
    k_nope, v = jnp.split(kv, [QK_NOPE], axis=-1)
    cos, sin = _rope_tables(s)
    q_rope = _rotate(q_rope, cos[None, :, None, :], sin[None, :, None, :])
    k_rope = _rotate(k_rope, cos[None], sin[None])
    attn_out = _mla_attention(q_nope, q_rope, k_nope, k_rope, v)

    mix = jnp.concatenate([conv_out, attn_out], axis=-1) @ w_out
    x = x + gt_m * _rmsnorm(mix, g_post_mix)

    h = _rmsnorm(x, g_pre_ffn) * (1 + sc_f) + sh_f
    gate, up = jnp.split(h @ w_gate_up, 2, axis=-1)
    f = (jax.nn.silu(gate) * up) @ w_down
    x = x + gt_f * _rmsnorm(f, g_post_ffn)
    return x


def setup_inputs(seed: int = 0) -> dict:
    key = jax.random.key(seed)
    ks = jax.random.split(key, 24)
    nrm = jax.random.normal
    f32 = jnp.float32

    def gain(k, n):
        return 1.0 + 0.02 * nrm(k, (DEPTH, n), f32)

    return {
        'x_prompt': nrm(ks[0], (BATCH, SEQ, D_MODEL), f32),
        'x_sample': nrm(ks[1], (DEC_BATCH, DEC_SEQ, D_MODEL), f32),
        'c_prompt': nrm(ks[2], (BATCH, D_MODEL), f32),
        'c_sample': nrm(ks[3], (DEC_BATCH, D_MODEL), f32),
        'w_ada': 0.5 * nrm(ks[4], (DEPTH, D_MODEL, 6 * D_MODEL), f32) * D_MODEL ** -0.5,
        'b_ada': 0.02 * nrm(ks[5], (DEPTH, 6 * D_MODEL), f32),
        'g_pre_mix': gain(ks[6], D_MODEL),
        'g_post_mix': gain(ks[7], D_MODEL),
        'w_in': nrm(ks[8], (DEPTH, D_MODEL, D_IN), f32) * D_MODEL ** -0.5,
        'w_dw': nrm(ks[9], (DEPTH, CONV_WIDTH, D_CONV), f32) * CONV_WIDTH ** -0.5,
        'b_dw': 0.02 * nrm(ks[10], (DEPTH, D_CONV), f32),
        'g_conv': gain(ks[11], D_CONV),
        'b_conv': 0.02 * nrm(ks[12], (DEPTH, D_CONV), f32),
        'g_q_lat': gain(ks[13], Q_LORA),
        'w_uq': nrm(ks[14], (DEPTH, Q_LORA, N_HEADS * (QK_NOPE + QK_ROPE)), f32) * Q_LORA ** -0.5,
        'g_kv_lat': gain(ks[15], KV_LORA),
        'w_ukv': nrm(ks[16], (DEPTH, KV_LORA, N_HEADS * (QK_NOPE + V_DIM)), f32) * KV_LORA ** -0.5,
        'w_out': nrm(ks[17], (DEPTH, D_MIX, D_MODEL), f32) * D_MIX ** -0.5,
        'g_pre_ffn': gain(ks[18], D_MODEL),
        'g_post_ffn': gain(ks[19], D_MODEL),
        'w_gate_up': nrm(ks[20], (DEPTH, D_MODEL, 2 * D_FF), f32) * D_MODEL ** -0.5,
        'w_down': nrm(ks[21], (DEPTH, D_FF, D_MODEL), f32) * D_FF ** -0.5,
    }


def reference(x_prompt, x_sample, c_prompt, c_sample, w_ada, b_ada, g_pre_mix, g_post_mix,
              w_in, w_dw, b_dw, g_conv, b_conv, g_q_lat, w_uq, g_kv_lat, w_ukv, w_out,
              g_pre_ffn, g_post_ffn, w_gate_up, w_down):
    y_prompt = x_prompt
    y_sample = x_sample
    for l in range(DEPTH):
        p = (w_ada[l], b_ada[l], g_pre_mix[l], g_post_mix[l], w_in[l], w_dw[l], b_dw[l],
             g_conv[l], b_conv[l], g_q_lat[l], w_uq[l], g_kv_lat[l], w_ukv[l], w_out[l],
             g_pre_ffn[l], g_post_ffn[l], w_gate_up[l], w_down[l])
        y_prompt = _layer(y_prompt, c_prompt, *p)
        y_sample = _layer(y_sample, c_sample, *p)
    return (y_prompt, y_sample)
```

```python
import functools

import jax
import jax.numpy as jnp
from jax import lax
from jax.experimental import pallas as pl
from jax.experimental.pallas import tpu as pltpu

F32 = jnp.float32
BF16 = jnp.bfloat16

D_MODEL = 2048
D_CONV = 1024
CONV_WIDTH = 31
CONV_PAD = CONV_WIDTH // 2
N_HEADS = 8
QK_NOPE = 128
QK_ROPE = 64
V_DIM = 128
Q_LORA = 512
KV_LORA = 512
D_FF = 5632
ROPE_BASE = 10000.0
EPS = 1e-6
SM_SCALE = (QK_NOPE + QK_ROPE) ** -0.5

LANE = 128
HEAD_PAD = 2 * LANE
HALO = 16
CONV_ROWS = 64
CONV_LANES = 256
VMEM_LIMIT = 56 * 1024 * 1024

U_KROPE = 2 * D_CONV + Q_LORA + KV_LORA
D_IN_PAD = U_KROPE + LANE


def _rms(x, g):
    return x * lax.rsqrt(jnp.mean(x * x, axis=-1, keepdims=True) + EPS) * g


def _resident(shape):
    n = len(shape)
    return pl.BlockSpec(shape, lambda *_: (0,) * n, pipeline_mode=pl.Buffered(1))


def _ada_kernel(c_ref, w_ref, b_ref, o_ref):
    c = c_ref[...]
    s = (c * jax.nn.sigmoid(c)).astype(BF16)
    o_ref[...] = jnp.dot(s, w_ref[...].astype(BF16), preferred_element_type=F32) + b_ref[...]


def _ada(c, w_ada, b_ada):
    rows, d = c.shape
    n = w_ada.shape[1]
    tn = 1024
    return pl.pallas_call(
        _ada_kernel,
        grid=(n // tn,),
        in_specs=[
            pl.BlockSpec((rows, d), lambda j: (0, 0)),
            pl.BlockSpec((d, tn), lambda j: (0, j)),
            pl.BlockSpec((1, tn), lambda j: (0, j)),
        ],
        out_specs=pl.BlockSpec((rows, tn), lambda j: (0, j)),
        out_shape=jax.ShapeDtypeStruct((rows, n), F32),
        compiler_params=pltpu.CompilerParams(
            dimension_semantics=("arbitrary",), vmem_limit_bytes=VMEM_LIMIT),
        name="ada",
    )(c, w_ada, b_ada.reshape(1, n))


def _inproj_kernel(x_ref, mod_ref, gpre_ref, win_ref, gq_ref, wuq_ref, gkv_ref, wukv_ref,
                   tq_ref, tk_ref, hglu_ref, q_ref, k_ref, vt_ref):
    x = x_ref[0]
    sh = mod_ref[0, 0:1, :]
    sc = mod_ref[0, 1:2, :]
    h = _rms(x, gpre_ref[...]) * (1.0 + sc) + sh
    u = jnp.dot(h.astype(BF16), win_ref[...], preferred_element_type=F32)

    hglu_ref[0] = u[:, :D_CONV] * jax.nn.sigmoid(u[:, D_CONV:2 * D_CONV])

    q_lat = u[:, 2 * D_CONV:2 * D_CONV + Q_LORA]
    kv_lat = u[:, 2 * D_CONV + Q_LORA:U_KROPE]
    qa = jnp.dot(_rms(q_lat, gq_ref[...]).astype(BF16), wuq_ref[...],
                 preferred_element_type=F32)
    kva = jnp.dot(_rms(kv_lat, gkv_ref[...]).astype(BF16), wukv_ref[...],
                  preferred_element_type=F32)

    kr = u[:, U_KROPE:D_IN_PAD] * tk_ref[...]
    kr = kr + pltpu.roll(kr, LANE // 2, 1)
    lane = lax.broadcasted_iota(jnp.int32, kr.shape, 1)
    kr = jnp.where(lane < QK_ROPE, kr, 0.0).astype(BF16)

    tq = tq_ref[...]
    for hd in range(N_HEADS):
        base = hd * HEAD_PAD
        q_ref[0, hd, :, 0:LANE] = (qa[:, base:base + LANE] * SM_SCALE).astype(BF16)
        t = qa[:, base + LANE:base + HEAD_PAD] * tq
        q_ref[0, hd, :, LANE:HEAD_PAD] = (t + pltpu.roll(t, LANE // 2, 1)).astype(BF16)
        k_ref[0, hd, :, 0:LANE] = kva[:, hd * LANE:(hd + 1) * LANE].astype(BF16)
        k_ref[0, hd, :, LANE:HEAD_PAD] = kr
        v = kva[:, N_HEADS * LANE + hd * V_DIM:N_HEADS * LANE + (hd + 1) * V_DIM]
        vt_ref[0, hd, 0] = v.T.astype(BF16)


def _inproj(x, mod, g_pre, w_in_p, g_q, w_uq_p, g_kv, w_ukv_p, tab_q, tab_k, *, tm, kc):
    b, s, d = x.shape
    nt = s // tm
    sub = kc // tm
    grid = (b, nt)
    return pl.pallas_call(
        _inproj_kernel,
        grid=grid,
        in_specs=[
            pl.BlockSpec((1, tm, d), lambda bi, si: (bi, si, 0)),
            pl.BlockSpec((1, 6, d), lambda bi, si: (bi, 0, 0)),
            _resident((1, d)),
            _resident(w_in_p.shape),
            _resident((1, Q_LORA)),
            _resident(w_uq_p.shape),
            _resident((1, KV_LORA)),
            _resident(w_ukv_p.shape),
            pl.BlockSpec((tm, LANE), lambda bi, si: (si, 0)),
            pl.BlockSpec((tm, LANE), lambda bi, si: (si, 0)),
        ],
        out_specs=[
            pl.BlockSpec((1, tm, D_CONV), lambda bi, si: (bi, si, 0)),
            pl.BlockSpec((1, N_HEADS, tm, HEAD_PAD), lambda bi, si: (bi, 0, si, 0)),
            pl.BlockSpec((1, N_HEADS, tm, HEAD_PAD), lambda bi, si: (bi, 0, si, 0)),
            pl.BlockSpec((1, N_HEADS, 1, V_DIM, tm), lambda bi, si: (bi, 0, si // sub, 0, si % sub)),
        ],
        out_shape=[
            jax.ShapeDtypeStruct((b, s, D_CONV), F32),
            jax.ShapeDtypeStruct((b, N_HEADS, s, HEAD_PAD), BF16),
            jax.ShapeDtypeStruct((b, N_HEADS, s, HEAD_PAD), BF16),
            jax.ShapeDtypeStruct((b, N_HEADS, s // kc, V_DIM, kc), BF16),
        ],
        compiler_params=pltpu.CompilerParams(
            dimension_semantics=("arbitrary", "arbitrary"), vmem_limit_bytes=VMEM_LIMIT),
        name="inproj",
    )(x, mod, g_pre, w_in_p, g_q, w_uq_p, g_kv, w_ukv_p, tab_q, tab_k)


def _attn_kernel(q_ref, k_ref, vt_ref, o_ref, acc_ref, m_ref, l_ref, *, tq_sub, kc):
    tq = q_ref.shape[2]
    n_kc = vt_ref.shape[2]

    def sub_body(j, carry):
        qoff = pl.multiple_of(j * tq_sub, tq_sub)
        q = q_ref[0, 0, pl.ds(qoff, tq_sub), :]
        m_ref[...] = jnp.full(m_ref.shape, -jnp.inf, F32)
        l_ref[...] = jnp.zeros(l_ref.shape, F32)
        acc_ref[...] = jnp.zeros(acc_ref.shape, F32)

        def kc_body(c, carry2):
            koff = pl.multiple_of(c * kc, kc)
            kblk = k_ref[0, 0, pl.ds(koff, kc), :]
            s = lax.dot_general(kblk, q, (((1,), (1,)), ((), ())),
                                preferred_element_type=F32)
            m_old = m_ref[...]
            m_new = jnp.maximum(m_old, jnp.max(s, axis=0, keepdims=True))
            alpha = jnp.exp(m_old - m_new)
            p = jnp.exp(s - m_new)
            l_ref[...] = alpha * l_ref[...] + jnp.sum(p, axis=0, keepdims=True)
            pv = jnp.dot(vt_ref[0, 0, c], p.astype(BF16), preferred_element_type=F32)
            acc_ref[...] = alpha * acc_ref[...] + pv
            m_ref[...] = m_new
            return carry2

        lax.fori_loop(0, n_kc, kc_body, 0)
        o = acc_ref[...] / l_ref[...]
        o_ref[0, pl.ds(qoff, tq_sub), :] = o.T.astype(BF16)
        return carry

    lax.fori_loop(0, tq // tq_sub, sub_body, 0)


def _attn(q, k, vt, *, tq, tq_sub):
    b, nh, s, _ = q.shape
    kc = vt.shape[-1]
    n_kc = vt.shape[2]
    return pl.pallas_call(
        functools.partial(_attn_kernel, tq_sub=tq_sub, kc=kc),
        grid=(b, nh, s // tq),
        in_specs=[
            pl.BlockSpec((1, 1, tq, HEAD_PAD), lambda bi, hi, qi: (bi, hi, qi, 0)),
            pl.BlockSpec((1, 1, s, HEAD_PAD), lambda bi, hi, qi: (bi, hi, 0, 0)),
            pl.BlockSpec((1, 1, n_kc, V_DIM, kc), lambda bi, hi, qi: (bi, hi, 0, 0, 0)),
        ],
        out_specs=pl.BlockSpec((1, tq, V_DIM), lambda bi, hi, qi: (bi, qi, hi)),
        out_shape=jax.ShapeDtypeStruct((b, s, nh * V_DIM), BF16),
        scratch_shapes=[
            pltpu.VMEM((V_DIM, tq_sub), F32),
            pltpu.VMEM((1, tq_sub), F32),
            pltpu.VMEM((1, tq_sub), F32),
        ],
        compiler_params=pltpu.CompilerParams(
            dimension_semantics=("arbitrary", "arbitrary", "arbitrary"),
            vmem_limit_bytes=VMEM_LIMIT),
        name="attn",
    )(q, k, vt)


def _mix_kernel(hc_ref, hp_ref, hn_ref, attn_ref, x_ref, mod_ref, wdw_ref, bdw_ref, gcv_ref,
                bcv_ref, wout_ref, gpost_ref, o_ref, win_ref, hcv_ref, conv_ref, *, rows, cg):
    tm = hc_ref.shape[1]
    si = pl.program_id(1)
    last = pl.num_programs(1) - 1
    win_ref[0:HALO, :] = jnp.where(si > 0, hp_ref[0], 0.0)
    win_ref[HALO:HALO + tm, :] = hc_ref[0]
    win_ref[HALO + tm:2 * HALO + tm, :] = jnp.where(si < last, hn_ref[0], 0.0)

    n = rows + 8
    assert HALO - CONV_PAD == 1

    def body(i, carry):
        r0 = pl.multiple_of(i * rows, rows)
        for c0 in range(0, D_CONV, cg):
            out = None
            for b in range(8):
                qb = None
                for a in range(4):
                    j = 8 * a + b
                    if j == 0 or j > CONV_WIDTH:
                        continue
                    term = (win_ref[pl.ds(r0 + 8 * a, n), c0:c0 + cg]
                            * wdw_ref[j - 1:j, c0:c0 + cg])
                    qb = term if qb is None else qb + term
                sh = qb[:rows] if b == 0 else pltpu.roll(qb, n - b, 0)[:rows]
                out = sh if out is None else out + sh
            hcv_ref[:, c0:c0 + cg] = out + bdw_ref[:, c0:c0 + cg]
        hcv = hcv_ref[...]
        mu = jnp.mean(hcv, axis=-1, keepdims=True)
        xc = hcv - mu
        y = xc * lax.rsqrt(jnp.mean(xc * xc, axis=-1, keepdims=True) + EPS)
        y = y * gcv_ref[...] + bcv_ref[...]
        conv_ref[pl.ds(r0, rows), :] = (y * jax.nn.sigmoid(y)).astype(BF16)
        return carry

    lax.fori_loop(0, tm // rows, body, 0)

    mix = (jnp.dot(conv_ref[...], wout_ref[0:D_CONV, :], preferred_element_type=F32)
           + jnp.dot(attn_ref[0], wout_ref[D_CONV:, :], preferred_element_type=F32))
    gt = mod_ref[0, 2:3, :]
    o_ref[0] = x_ref[0] + gt * _rms(mix, gpost_ref[...])


def _mix(hglu, attn, x, mod, w_dw, b_dw, g_conv, b_conv, w_out, g_post, *, tm):
    b, s, d = x.shape
    nt = s // tm
    hb = tm // HALO
    n_hblk = s // HALO
    return pl.pallas_call(
        functools.partial(_mix_kernel, rows=CONV_ROWS, cg=CONV_LANES),
        grid=(b, nt),
        in_specs=[
            pl.BlockSpec((1, tm, D_CONV), lambda bi, si: (bi, si, 0)),
            pl.BlockSpec((1, HALO, D_CONV), lambda bi, si: (bi, jnp.maximum(si * hb - 1, 0), 0)),
            pl.BlockSpec((1, HALO, D_CONV),
                         lambda bi, si: (bi, jnp.minimum((si + 1) * hb, n_hblk - 1), 0)),
            pl.BlockSpec((1, tm, D_CONV), lambda bi, si: (bi, si, 0)),
            pl.BlockSpec((1, tm, d), lambda bi, si: (bi, si, 0)),
            pl.BlockSpec((1, 6, d), lambda bi, si: (bi, 0, 0)),
            _resident((CONV_WIDTH, D_CONV)),
            _resident((1, D_CONV)),
            _resident((1, D_CONV)),
            _resident((1, D_CONV)),
            _resident(w_out.shape),
            _resident((1, d)),
        ],
        out_specs=pl.BlockSpec((1, tm, d), lambda bi, si: (bi, si, 0)),
        out_shape=jax.ShapeDtypeStruct((b, s, d), F32),
        scratch_shapes=[
            pltpu.VMEM((tm + 2 * HALO, D_CONV), F32),
            pltpu.VMEM((CONV_ROWS, D_CONV), F32),
            pltpu.VMEM((tm, D_CONV), BF16),
        ],
        compiler_params=pltpu.CompilerParams(
            dimension_semantics=("arbitrary", "arbitrary"), vmem_limit_bytes=VMEM_LIMIT),
        name="mix",
    )(hglu, hglu, hglu, attn, x, mod, w_dw, b_dw, g_conv, b_conv, w_out, g_post)


def _ffn_kernel(x_ref, mod_ref, gpre_ref, wg_ref, wu_ref, wd_ref, gpost_ref, o_ref, h_ref, acc_ref):
    f = pl.program_id(2)

    @pl.when(f == 0)
    def _():
        sh = mod_ref[0, 3:4, :]
        sc = mod_ref[0, 4:5, :]
        h_ref[...] = (_rms(x_ref[0], gpre_ref[...]) * (1.0 + sc) + sh).astype(BF16)
        acc_ref[...] = jnp.zeros(acc_ref.shape, F32)

    h = h_ref[...]
    g = jnp.dot(h, wg_ref[...], preferred_element_type=F32)
    u = jnp.dot(h, wu_ref[...], preferred_element_type=F32)
    a = (g * jax.nn.sigmoid(g) * u).astype(BF16)
    acc_ref[...] += jnp.dot(a, wd_ref[...], preferred_element_type=F32)

    @pl.when(f == pl.num_programs(2) - 1)
    def _():
        gt = mod_ref[0, 5:6, :]
        o_ref[0] = x_ref[0] + gt * _rms(acc_ref[...], gpost_ref[...])


def _ffn(x, mod, g_pre, w_gate_up, w_down, g_post, *, tm, tf):
    b, s, d = x.shape
    nf = D_FF // tf
    return pl.pallas_call(
        _ffn_kernel,
        grid=(b, s // tm, nf),
        in_specs=[
            pl.BlockSpec((1, tm, d), lambda bi, si, fi: (bi, si, 0)),
            pl.BlockSpec((1, 6, d), lambda bi, si, fi: (bi, 0, 0)),
            _resident((1, d)),
            pl.BlockSpec((d, tf), lambda bi, si, fi: (0, fi)),
            pl.BlockSpec((d, tf), lambda bi, si, fi: (0, fi + nf)),
            pl.BlockSpec((tf, d), lambda bi, si, fi: (fi, 0)),
            _resident((1, d)),
        ],
        out_specs=pl.BlockSpec((1, tm, d), lambda bi, si, fi: (bi, si, 0)),
        out_shape=jax.ShapeDtypeStruct((b, s, d), F32),
        scratch_shapes=[
            pltpu.VMEM((tm, d), BF16),
            pltpu.VMEM((tm, d), F32),
        ],
        compiler_params=pltpu.CompilerParams(
            dimension_semantics=("arbitrary", "arbitrary", "arbitrary"),
            vmem_limit_bytes=VMEM_LIMIT),
        name="ffn",
    )(x, mod, g_pre, w_gate_up, w_gate_up, w_down, g_post)


def _rotate_half_twin(w):
    half = w.shape[1] // 2
    return jnp.concatenate([-w[:, half:], w[:, :half]], axis=1)


def _prep_weights(w_in, w_uq, w_ukv, w_out, w_gate_up, w_down):
    w_in_p = jnp.concatenate([w_in, _rotate_half_twin(w_in[:, U_KROPE:])], axis=1).astype(BF16)
    wq = w_uq.reshape(Q_LORA, N_HEADS, QK_NOPE + QK_ROPE)
    rope = wq[:, :, QK_NOPE:]
    twin = jnp.concatenate([-rope[:, :, QK_ROPE // 2:], rope[:, :, :QK_ROPE // 2]], axis=2)
    w_uq_p = jnp.concatenate([wq, twin], axis=2).reshape(Q_LORA, N_HEADS * HEAD_PAD).astype(BF16)
    wkv = w_ukv.reshape(KV_LORA, N_HEADS, QK_NOPE + V_DIM)
    w_ukv_p = jnp.concatenate(
        [wkv[:, :, :QK_NOPE].reshape(KV_LORA, N_HEADS * QK_NOPE),
         wkv[:, :, QK_NOPE:].reshape(KV_LORA, N_HEADS * V_DIM)], axis=1).astype(BF16)
    return (w_in_p, w_uq_p, w_ukv_p, w_out.astype(BF16), w_gate_up.astype(BF16),
            w_down.astype(BF16))


def _rope_table(seq):
    inv = 1.0 / (ROPE_BASE ** (jnp.arange(0, QK_ROPE, 2, dtype=F32) / QK_ROPE))
    ang = jnp.arange(seq, dtype=F32)[:, None] * inv[None, :]
    cos, sin = jnp.cos(ang), jnp.sin(ang)
    return jnp.concatenate([cos, cos, sin, sin], axis=1)


def _layer(x, mod, p):
    (g_pre_mix, g_post_mix, w_in_p, w_dw, b_dw, g_conv, b_conv, g_q_lat, w_uq_p, g_kv_lat,
     w_ukv_p, w_out, g_pre_ffn, g_post_ffn, w_gate_up, w_down) = p
    s = x.shape[1]
    tab = _rope_table(s)
    hglu, q, k, vt = _inproj(x, mod, g_pre_mix, w_in_p, g_q_lat, w_uq_p, g_kv_lat, w_ukv_p,
                             tab * SM_SCALE, tab, tm=256, kc=512)
    attn = _attn(q, k, vt, tq=2048, tq_sub=1024)
    x1 = _mix(hglu, attn, x, mod, w_dw, b_dw, g_conv, b_conv, w_out, g_post_mix, tm=512)
    return _ffn(x1, mod, g_pre_ffn, w_gate_up, w_down, g_post_ffn, tm=512, tf=512)


def kernel(x_prompt, x_sample, c_prompt, c_sample, w_ada, b_ada, g_pre_mix, g_post_mix, w_in, w_dw,
           b_dw, g_conv, b_conv, g_q_lat, w_uq, g_kv_lat, w_ukv, w_out, g_pre_ffn, g_post_ffn,
           w_gate_up, w_down):
    depth = w_ada.shape[0]
    nb_p, nb_s = c_prompt.shape[0], c_sample.shape[0]
    pad = (-(nb_p + nb_s)) % 8
    y_prompt, y_sample = x_prompt, x_sample
    for l in range(depth):
        c = jnp.concatenate([c_prompt, c_sample, jnp.zeros((pad, D_MODEL), F32)], axis=0)
        mod = _ada(c, w_ada[l], b_ada[l]).reshape(-1, 6, D_MODEL)
        w_in_p, w_uq_p, w_ukv_p, w_out_b, w_gu_b, w_down_b = _prep_weights(
            w_in[l], w_uq[l], w_ukv[l], w_out[l], w_gate_up[l], w_down[l])
        row = lambda v: v[l].reshape(1, -1)
        p = (row(g_pre_mix), row(g_post_mix), w_in_p, w_dw[l], row(b_dw), row(g_conv), row(b_conv),
             row(g_q_lat), w_uq_p, row(g_kv_lat), w_ukv_p, w_out_b, row(g_pre_ffn),
             row(g_post_ffn), w_gu_b, w_down_b)
        y_prompt = _layer(y_prompt, mod[:nb_p], p)
        y_sample = _layer(y_sample, mod[nb_p:nb_p + nb_s], p)
    return (y_prompt, y_sample)
```
